```python
import math, functools
import jax
import jax.numpy as jnp
from jax import lax
import numpy as np

D_MODEL = 1024
BATCH = 32
SEQ = 2048
DEPTH = 4

GRID_W = 64
CTX_LEN = 256
N_BRANCH = 4
BRANCH_WIDTH = D_MODEL // 2
ATT_HEAD_DIM = 64
ATT_HEADS = BRANCH_WIDTH // ATT_HEAD_DIM
ATT_KV_HEADS = ATT_HEADS // 4
Q_BLOCK = 128
ROPE_THETA = 10000.0
DN_HEAD_DIM = 128
DN_HEADS = BRANCH_WIDTH // DN_HEAD_DIM
DN_CONV = 3
RET_HEADS = 4
RET_V_DIM = BRANCH_WIDTH // RET_HEADS
RET_K_DIM = RET_V_DIM // 2
SC_WIDTH = BRANCH_WIDTH
SC_CONV = 3
CHUNK = 64
MLP_HIDDEN = 4 * D_MODEL
EPS = 1e-6

ATT_Q_W = ATT_HEADS * ATT_HEAD_DIM
ATT_KV_W = ATT_KV_HEADS * ATT_HEAD_DIM
DN_W = DN_HEADS * DN_HEAD_DIM
RET_QK_W = RET_HEADS * RET_K_DIM
RET_V_W = RET_HEADS * RET_V_DIM
IN_SPLITS = (ATT_Q_W, ATT_KV_W, ATT_KV_W,
             DN_W, DN_W, DN_W, DN_W,
             DN_HEADS, DN_HEADS, DN_HEADS, DN_HEADS,
             RET_QK_W, RET_QK_W, RET_V_W, RET_V_W,
             SC_WIDTH, SC_WIDTH, SC_WIDTH,
             N_BRANCH * D_MODEL)
N_IN = sum(IN_SPLITS)

kernel_name = 'hybrid_parallel_diffusion_trunk'


def rms_norm(x, gain):
    xf = x.astype(jnp.float32)
    y = xf * lax.rsqrt(jnp.mean(xf * xf, axis=-1, keepdims=True) + EPS)
    return (y * gain.astype(jnp.float32)).astype(x.dtype)


def l2_norm(x):
    xf = x.astype(jnp.float32)
    return xf * lax.rsqrt(jnp.sum(xf * xf, axis=-1, keepdims=True) + EPS)


def head_norm(o):
    mu = jnp.mean(o, axis=-1, keepdims=True)
    var = jnp.mean(jnp.square(o - mu), axis=-1, keepdims=True)
    return (o - mu) * lax.rsqrt(var + EPS)


def rope_tables(rows, head_dim):
    r, col = jnp.meshgrid(jnp.arange(rows), jnp.arange(GRID_W), indexing='ij')
    quarter = head_dim // 4
    inv_freq = ROPE_THETA ** (-jnp.arange(quarter, dtype=jnp.float32) / quarter)
    ang = jnp.concatenate([r.reshape(-1, 1).astype(jnp.float32) * inv_freq,
                           col.reshape(-1, 1).astype(jnp.float32) * inv_freq], axis=-1)
    return jnp.cos(ang), jnp.sin(ang)


def apply_rope(x, cos, sin):
    half = x.shape[-1] // 2
    x1, x2 = x[..., :half], x[..., half:]
    c = cos[None, :, None, :].astype(x.dtype)
    s = sin[None, :, None, :].astype(x.dtype)
    return jnp.concatenate([x1 * c - x2 * s, x1 * s + x2 * c], axis=-1)


def dw_conv(x, w):
    width = w.shape[0]
    return lax.conv_general_dilated(x, w[:, None, :].astype(x.dtype), window_strides=(1,),
                                    padding=[(width // 2, width // 2)],
                                    dimension_numbers=('NWC', 'WIO', 'NWC'),
                                    feature_group_count=x.shape[-1])


def project_in(h, w_in, n_parts):
    bounds = np.cumsum(IN_SPLITS)[:-1].tolist()
    return [h @ w for w in jnp.split(w_in, bounds, axis=1)[:n_parts]]


def _sdpa(q, k, v):
    b, nq, hq, dh = q.shape
    hkv = k.shape[2]
    qg = q.reshape(b, nq, hkv, hq // hkv, dh)
    s = jnp.einsum('bqhgd,bkhd->bhgqk', qg, k).astype(jnp.float32) * (dh ** -0.5)
    p = jax.nn.softmax(s, axis=-1).astype(v.dtype)
    o = jnp.einsum('bhgqk,bkhd->bqhgd', p, v)
    return o.reshape(b, nq, hq * dh)


def attention_mixer(lat, ctx, q_gain, k_gain, rope, need_ctx):
    def heads(t, n_heads):
        return t.reshape(t.shape[0], t.shape[1], n_heads, ATT_HEAD_DIM)
    q_l, k_l, v_l = lat
    q_c, k_c, v_c = ctx
    bsz, n, _ = q_l.shape
    q_l = apply_rope(rms_norm(heads(q_l, ATT_HEADS), q_gain), *rope)
    k_l = apply_rope(rms_norm(heads(k_l, ATT_KV_HEADS), k_gain), *rope)
    q_c = rms_norm(heads(q_c, ATT_HEADS), q_gain)
    k_c = rms_norm(heads(k_c, ATT_KV_HEADS), k_gain)
    v_l, v_c = heads(v_l, ATT_KV_HEADS), heads(v_c, ATT_KV_HEADS)
    keys = jnp.concatenate([k_c, k_l], axis=1)
    vals = jnp.concatenate([v_c, v_l], axis=1)
    q_blocks = jnp.moveaxis(q_l.reshape(bsz, n // Q_BLOCK, Q_BLOCK, ATT_HEADS, ATT_HEAD_DIM), 1, 0)
    o_blocks = lax.map(lambda qb: _sdpa(qb, keys, vals), q_blocks)
    y_l = jnp.moveaxis(o_blocks, 0, 1).reshape(bsz, n, ATT_Q_W)
    y_c = _sdpa(q_c, k_c, v_c) if need_ctx else None
    return y_l, y_c


def _to_chunks(t, n):
    t = t.astype(jnp.float32).reshape((t.shape[0], n, CHUNK) + t.shape[2:])
    return jnp.moveaxis(t, 2, 3)


def _from_chunks(t):
    t = jnp.moveaxis(t, 3, 2)
    return t.reshape(t.shape[0], t.shape[1] * t.shape[2], t.shape[3], t.shape[4])


def gated_delta_rule(q, k, v, log_a, beta, s0):
    bsz, length, _, _ = q.shape
    dv = v.shape[-1]
    n = length // CHUNK
    q, k, v = _to_chunks(q, n), _to_chunks(k, n), _to_chunks(v, n)
    log_a, beta = _to_chunks(log_a, n), _to_chunks(beta, n)
    g = jnp.cumsum(log_a, axis=-1)
    diff = g[..., :, None] - g[..., None, :]
    incl = jnp.tril(jnp.ones((CHUNK, CHUNK), dtype=bool))
    strict = jnp.tril(jnp.ones((CHUNK, CHUNK), dtype=bool), -1)
    dec_incl = jnp.exp(jnp.where(incl, diff, -jnp.inf))
    dec_strict = jnp.where(strict, dec_incl, 0.0)
    kk = jnp.einsum('bnhid,bnhjd->bnhij', k, k)
    a_mat = beta[..., :, None] * kk * dec_strict + jnp.eye(CHUNK, dtype=jnp.float32)
    rhs = jnp.concatenate([beta[..., None] * v, (beta * jnp.exp(g))[..., None] * k], axis=-1)
    sol = lax.linalg.triangular_solve(a_mat, rhs, left_side=True, lower=True, unit_diagonal=True)
    w_v, w_k = sol[..., :dv], sol[..., dv:]
    a_qk = jnp.einsum('bnhid,bnhjd->bnhij', q, k) * dec_incl
    q_dec = q * jnp.exp(g)[..., None]
    g_last = g[..., -1]
    k_dec = k * jnp.exp(g_last[..., None] - g)[..., None]

    def step(s, inp):
        w_v_i, w_k_i, a_i, qd_i, kd_i, gl_i = inp
        u = w_v_i - jnp.einsum('bhcd,bhde->bhce', w_k_i, s)
        o = jnp.einsum('bhcd,bhde->bhce', qd_i, s) + jnp.einsum('bhij,bhje->bhie', a_i, u)
        s = s * jnp.exp(gl_i)[..., None, None] + jnp.einsum('bhcd,bhce->bhde', kd_i, u)
        return s, o

    xs = tuple(jnp.moveaxis(t, 1, 0) for t in (w_v, w_k, a_qk, q_dec, k_dec, g_last))
    s_fin, o = lax.scan(step, s0.astype(jnp.float32), xs)
    return _from_chunks(jnp.moveaxis(o, 0, 1)), s_fin


def retention_chunked(q, k, v, s0, log_g):
    length = q.shape[1]
    n = length // CHUNK
    q, k, v = _to_chunks(q, n), _to_chunks(k, n), _to_chunks(v, n)
    idx = jnp.arange(CHUNK, dtype=jnp.float32)
    lg = log_g.astype(jnp.float32)[:, None]
    rel = idx[:, None] - idx[None, :]
    dmask = jnp.exp(jnp.where(rel >= 0, rel[None] * lg[:, :, None], -jnp.inf))
    scores = jnp.einsum('bnhid,bnhjd->bnhij', q, k) * dmask
    o = jnp.einsum('bnhij,bnhje->bnhie', scores, v)
    k_dec = k * jnp.exp((CHUNK - 1 - idx)[None, :] * lg)[..., None]
    kv = jnp.einsum('bnhcd,bnhce->bnhde', k_dec, v)
    chunk_decay = jnp.exp(CHUNK * lg)[:, :, None]

    def step(s, kv_i):
        return s * chunk_decay + kv_i, s

    s_fin, s_prev = lax.scan(step, s0.astype(jnp.float32), jnp.moveaxis(kv, 1, 0))
    q_dec = q * jnp.exp((idx + 1)[None, :] * lg)[..., None]
    o = o + jnp.einsum('bnhcd,bnhde->bnhce', q_dec, jnp.moveaxis(s_prev, 0, 1))
    return _from_chunks(o), s_fin


def _ctx_then_latent(fn, ctx_args, lat_args, s0, reverse):
    def flip(ts):
        return tuple(jnp.flip(t, axis=1) for t in ts) if reverse else tuple(ts)
    o_c, s_c = fn(*flip(ctx_args), s0)
    o_l, _ = fn(*flip(lat_args), s_c)
    if reverse:
        o_c, o_l = jnp.flip(o_c, axis=1), jnp.flip(o_l, axis=1)
    return o_l, o_c


def deltanet_mixer(lat, ctx, conv_w, a_log, dt_bias, norm_gain, need_ctx):
    a_coef = -jnp.exp(a_log.astype(jnp.float32))

    def prep(q, k, v, z, a_f, a_b, b_f, b_b):
        bsz, n, _ = q.shape
        qkv = jax.nn.silu(dw_conv(jnp.concatenate([q, k, v], axis=-1), conv_w))
        q, k, v = jnp.split(qkv, 3, axis=-1)
        q = l2_norm(q.reshape(bsz, n, DN_HEADS, DN_HEAD_DIM)) * (DN_HEAD_DIM ** -0.5)
        k = l2_norm(k.reshape(bsz, n, DN_HEADS, DN_HEAD_DIM))
        v = v.reshape(bsz, n, DN_HEADS, DN_HEAD_DIM)
        dirs = []
        for d, (a_in, b_in) in enumerate(((a_f, b_f), (a_b, b_b))):
            log_a = a_coef[d] * jax.nn.softplus(a_in.astype(jnp.float32) + dt_bias[d].astype(jnp.float32))
            dirs.append((q, k, v, log_a, jax.nn.sigmoid(b_in.astype(jnp.float32))))
        return dirs, z

    lat_dirs, z_l = prep(*lat)
    ctx_dirs, z_c = prep(*ctx)
    bsz = z_l.shape[0]
    s0 = jnp.zeros((bsz, DN_HEADS, DN_HEAD_DIM, DN_HEAD_DIM), jnp.float32)
    fwd = _ctx_then_latent(gated_delta_rule, ctx_dirs[0], lat_dirs[0], s0, False)
    bwd = _ctx_then_latent(gated_delta_rule, ctx_dirs[1], lat_dirs[1], s0, True)

    def finish(o, z):
        zf = jax.nn.silu(z.astype(jnp.float32)).reshape(o.shape)
        y = rms_norm(o, norm_gain) * zf
        return y.reshape(o.shape[0], o.shape[1], DN_W).astype(z.dtype)

    y_l = finish(fwd[0] + bwd[0], z_l)
    y_c = finish(fwd[1] + bwd[1], z_c) if need_ctx else None
    return y_l, y_c


def retention_mixer(lat, ctx, ret_decay, rope, need_ctx):
    def prep(q, k, v, g, with_pos):
        bsz, n, _ = q.shape
        q = q.reshape(bsz, n, RET_HEADS, RET_K_DIM)
        k = k.reshape(bsz, n, RET_HEADS, RET_K_DIM) * (RET_K_DIM ** -0.5)
        if with_pos:
            q, k = apply_rope(q, *rope), apply_rope(k, *rope)
        v = v.reshape(bsz, n, RET_HEADS, RET_V_DIM)
        return (q, k, v), g

    lat_args, g_l = prep(*lat, True)
    ctx_args, g_c = prep(*ctx, False)
    log_g = -jnp.exp(ret_decay.astype(jnp.float32))
    bsz = g_l.shape[0]
    s0 = jnp.zeros((bsz, RET_HEADS, RET_K_DIM, RET_V_DIM), jnp.float32)
    fwd = _ctx_then_latent(functools.partial(retention_chunked, log_g=log_g[0]), ctx_args, lat_args, s0, False)
    bwd = _ctx_then_latent(functools.partial(retention_chunked, log_g=log_g[1]), ctx_args, lat_args, s0, True)

    def finish(o, g):
        y = head_norm(o).reshape(o.shape[0], o.shape[1], RET_V_W) * jax.nn.silu(g.astype(jnp.float32))
        return y.astype(g.dtype)

    y_l = finish(fwd[0] + bwd[0], g_l)
    y_c = finish(fwd[1] + bwd[1], g_c) if need_ctx else None
    return y_l, y_c


def shortconv_mixer(b_gate, c_gate, x_in, w):
    return b_gate * dw_conv(c_gate * x_in, w)


def merge_branches(ys, gates, w_branch, w_out):
    g = jax.nn.sigmoid(gates.reshape(gates.shape[:-1] + (N_BRANCH, D_MODEL)))
    u = g[..., 0, :] * (ys[0] @ w_branch[0])
    for i in range(1, N_BRANCH):
        u = u + g[..., i, :] * (ys[i] @ w_branch[i])
    return u @ w_out


def sq_relu_mlp(h, w1, w2):
    return jnp.square(jax.nn.relu(h @ w1)) @ w2


def layer(x, xc, c, c_ctx, w_mod, b_mod, g_norm, w_in, att_q_gain, att_k_gain, dn_conv, dn_a_log,
          dn_dt_bias, dn_norm_gain, ret_decay, sc_conv, w_branch, w_out, w_mlp_in, w_mlp_out,
          att_rope, ret_rope, need_ctx):
    mod = jax.nn.silu(c) @ w_mod + b_mod
    mod_c = jax.nn.silu(c_ctx) @ w_mod + b_mod
    sh1, sc1, ga1, sh2, sc2, ga2 = jnp.split(mod[:, None, :], 6, axis=-1)
    sh1c, sc1c, ga1c, sh2c, sc2c, ga2c = jnp.split(mod_c, 6, axis=-1)

    h = rms_norm(x, g_norm[0]) * (1 + sc1) + sh1
    hc = rms_norm(xc, g_norm[0]) * (1 + sc1c) + sh1c
    zl = project_in(h, w_in, len(IN_SPLITS))
    zc = project_in(hc, w_in, len(IN_SPLITS) if need_ctx else 15)
    y_att, yc_att = attention_mixer(zl[0:3], zc[0:3], att_q_gain, att_k_gain, att_rope, need_ctx)
    y_dn, yc_dn = deltanet_mixer(zl[3:11], zc[3:11], dn_conv, dn_a_log, dn_dt_bias, dn_norm_gain, need_ctx)
    y_ret, yc_ret = retention_mixer(zl[11:15], zc[11:15], ret_decay, ret_rope, need_ctx)
    y_sc = shortconv_mixer(zl[15], zl[16], zl[17], sc_conv)
    y = merge_branches((y_att, y_dn, y_ret, y_sc), zl[18], w_branch, w_out)
    x = x + ga1 * rms_norm(y, g_norm[1])

    h2 = rms_norm(x, g_norm[2]) * (1 + sc2) + sh2
    x = x + ga2 * rms_norm(sq_relu_mlp(h2, w_mlp_in, w_mlp_out), g_norm[3])

    if need_ctx:
        yc_sc = shortconv_mixer(zc[15], zc[16], zc[17], sc_conv)
        yc = merge_branches((yc_att, yc_dn, yc_ret, yc_sc), zc[18], w_branch, w_out)
        xc = xc + ga1c * rms_norm(yc, g_norm[1])
        h2c = rms_norm(xc, g_norm[2]) * (1 + sc2c) + sh2c
        xc = xc + ga2c * rms_norm(sq_relu_mlp(h2c, w_mlp_in, w_mlp_out), g_norm[3])
    return x, xc


def setup_inputs(seed: int = 0) -> dict:
    key = jax.random.key(seed)
    ks = jax.random.split(key, 24)
    f32 = jnp.float32

    def nrm(k, shape, scale):
        return jax.random.normal(k, shape, f32) * scale

    x = nrm(ks[0], (BATCH, SEQ, D_MODEL), 1.0)
    c = nrm(ks[1], (BATCH, D_MODEL), 1.0)
    ctx = nrm(ks[2], (BATCH, CTX_LEN, D_MODEL), 1.0)
    c_ctx = nrm(ks[3], (D_MODEL,), 1.0)
    w_mod = nrm(ks[4], (DEPTH, D_MODEL, 6 * D_MODEL), 0.5 * D_MODEL ** -0.5)
    b_mod = nrm(ks[5], (DEPTH, 6 * D_MODEL), 0.01)
    g_norm = 1.0 + nrm(ks[6], (DEPTH, 4, D_MODEL), 0.05)
    w_in = nrm(ks[7], (DEPTH, D_MODEL, N_IN), D_MODEL ** -0.5)
    att_q_gain = 1.0 + nrm(ks[8], (DEPTH, ATT_HEAD_DIM), 0.05)
    att_k_gain = 1.0 + nrm(ks[9], (DEPTH, ATT_HEAD_DIM), 0.05)
    dn_conv = nrm(ks[10], (DEPTH, DN_CONV, 3 * DN_W), DN_CONV ** -0.5)
    dn_a_log = jnp.log(jax.random.uniform(ks[11], (DEPTH, 2, DN_HEADS), f32, 1.0, 16.0))
    dt = jnp.exp(jax.random.uniform(ks[12], (DEPTH, 2, DN_HEADS), f32, math.log(1e-3), math.log(0.1)))
    dn_dt_bias = dt + jnp.log(-jnp.expm1(-dt))
    dn_norm_gain = 1.0 + nrm(ks[13], (DEPTH, DN_HEAD_DIM), 0.05)
    base = jnp.log(-jnp.log(1.0 - 2.0 ** (-5.0 - jnp.arange(RET_HEADS, dtype=f32))))
    ret_decay = base + nrm(ks[14], (DEPTH, 2, RET_HEADS), 0.05)
    sc_conv = nrm(ks[15], (DEPTH, SC_CONV, SC_WIDTH), SC_CONV ** -0.5)
    w_branch = nrm(ks[16], (DEPTH, N_BRANCH, BRANCH_WIDTH, D_MODEL), BRANCH_WIDTH ** -0.5)
    w_out = nrm(ks[17], (DEPTH, D_MODEL, D_MODEL), D_MODEL ** -0.5)
    w_mlp_in = nrm(ks[18], (DEPTH, D_MODEL, MLP_HIDDEN), D_MODEL ** -0.5)
    w_mlp_out = nrm(ks[19], (DEPTH, MLP_HIDDEN, D_MODEL), MLP_HIDDEN ** -0.5)
    return {'x': x, 'c': c, 'ctx': ctx, 'c_ctx': c_ctx, 'w_mod': w_mod, 'b_mod': b_mod,
            'g_norm': g_norm, 'w_in': w_in, 'att_q_gain': att_q_gain, 'att_k_gain': att_k_gain,
            'dn_conv': dn_conv, 'dn_a_log': dn_a_log, 'dn_dt_bias': dn_dt_bias,
            'dn_norm_gain': dn_norm_gain, 'ret_decay': ret_decay, 'sc_conv': sc_conv,
            'w_branch': w_branch, 'w_out': w_out, 'w_mlp_in': w_mlp_in, 'w_mlp_out': w_mlp_out}


def reference(x, c, ctx, c_ctx, w_mod, b_mod, g_norm, w_in, att_q_gain, att_k_gain, dn_conv,
              dn_a_log, dn_dt_bias, dn_norm_gain, ret_decay, sc_conv, w_branch, w_out,
              w_mlp_in, w_mlp_out):
    ROWS = x.shape[1] // GRID_W
    att_rope = rope_tables(ROWS, ATT_HEAD_DIM)
    ret_rope = rope_tables(ROWS, RET_K_DIM)
    xc = ctx
    for l in range(DEPTH):
        x, xc = layer(x, xc, c, c_ctx, w_mod[l], b_mod[l], g_norm[l], w_in[l], att_q_gain[l],
                      att_k_gain[l], dn_conv[l], dn_a_log[l], dn_dt_bias[l], dn_norm_gain[l],
                      ret_decay[l], sc_conv[l], w_branch[l], w_out[l], w_mlp_in[l], w_mlp_out[l],
                      att_rope, ret_rope, l < DEPTH - 1)
    return x
```

```python
import functools
import math

import jax
import jax.numpy as jnp
import numpy as np
from jax import lax
from jax.experimental import pallas as pl
from jax.experimental.pallas import tpu as pltpu

F32 = jnp.float32
BF16 = jnp.bfloat16

D_MODEL = 1024
GRID_W = 64
N_BRANCH = 4
BRANCH_WIDTH = D_MODEL // 2
ATT_HEAD_DIM = 64
ATT_HEADS = 8
ATT_KV_HEADS = 2
ROPE_THETA = 10000.0
DN_HEAD_DIM = 128
DN_HEADS = 4
RET_HEADS = 4
RET_V_DIM = 128
RET_K_DIM = 64
CHUNK = 64
MLP_HIDDEN = 4 * D_MODEL
EPS = 1e-6

LANES = 128
SUBLANES = 8
HALO = SUBLANES
VMEM_LIMIT = 56 * 1024 * 1024

_SPLITS = (512, 128, 128, 512, 512, 512, 512, 4, 4, 4, 4, 256, 256, 512, 512, 512, 512, 512,
           N_BRANCH * D_MODEL)
_OFFS = np.concatenate([[0], np.cumsum(_SPLITS)]).tolist()


def _dot(a, b):
    return jnp.dot(a.astype(BF16), b.astype(BF16), preferred_element_type=F32)


def _dot_nt(a, b):
    return lax.dot_general(a.astype(BF16), b.astype(BF16), (((1,), (1,)), ((), ())),
                           preferred_element_type=F32)


def _dot_tn(a, b):
    return lax.dot_general(a.astype(BF16), b.astype(BF16), (((0,), (0,)), ((), ())),
                           preferred_element_type=F32)


def _split2(x):
    hi = x.astype(BF16)
    lo = (x - hi.astype(F32)).astype(BF16)
    return hi, lo


def _split3(x):
    hi = x.astype(BF16)
    r = x - hi.astype(F32)
    mid = r.astype(BF16)
    lo = (r - mid.astype(F32)).astype(BF16)
    return hi, mid, lo


def _dot_exact_lhs(a_bf16, x):
    hi, mid, lo = _split3(x)
    d = functools.partial(jnp.dot, preferred_element_type=F32)
    return d(a_bf16, hi) + d(a_bf16, mid) + d(a_bf16, lo)


def _dot_x3(a, b):
    ah, al = _split2(a)
    bh, bl = _split2(b)
    d = functools.partial(jnp.dot, preferred_element_type=F32)
    return d(ah, bh) + (d(ah, bl) + d(al, bh))


def _silu(x):
    return x * jax.nn.sigmoid(x)


def _softplus(x):
    return jnp.maximum(x, 0.0) + jnp.log(1.0 + jnp.exp(-jnp.abs(x)))


def _const_spec(shape):
    nd = len(shape)
    return pl.BlockSpec(shape, lambda *_: (0,) * nd)


def _params(sem):
    return pltpu.CompilerParams(dimension_semantics=sem, vmem_limit_bytes=VMEM_LIMIT)


def _mod_kernel(c_ref, w_ref, b_ref, o_ref):
    s = _silu(c_ref[...])
    w = w_ref[...]
    sh, sl = _split2(s)
    wh, wl = _split2(w)
    d = functools.partial(jnp.dot, preferred_element_type=F32)
    o_ref[...] = d(sh, wh) + (d(sh, wl) + d(sl, wh)) + b_ref[...]


def _modulation(cc, w_mod, b_mod):
    depth, _, n = w_mod.shape
    rows = cc.shape[0]
    tn = 1536
    return pl.pallas_call(
        _mod_kernel,
        out_shape=jax.ShapeDtypeStruct((depth, rows, n), F32),
        grid=(depth, n // tn),
        in_specs=[pl.BlockSpec((rows, D_MODEL), lambda l, j: (0, 0)),
                  pl.BlockSpec((None, D_MODEL, tn), lambda l, j: (l, 0, j)),
                  pl.BlockSpec((None, 1, tn), lambda l, j: (l, 0, j))],
        out_specs=pl.BlockSpec((None, rows, tn), lambda l, j: (l, 0, j)),
        compiler_params=_params(("arbitrary", "arbitrary")),
        name="modulation",
    )(cc, w_mod, b_mod.reshape(depth, 1, n))


def _swap_halves64(x):
    lane = lax.broadcasted_iota(jnp.int32, (1, LANES), 1)
    first = (lane % 64) < 32
    cols = []
    for j in range(x.shape[1] // LANES):
        xb = x[:, j * LANES:(j + 1) * LANES]
        cols.append(jnp.where(first, pltpu.roll(xb, LANES - 32, 1), pltpu.roll(xb, 32, 1)))
    return jnp.concatenate(cols, axis=1)


def _conv3(z_ref, w, tm):
    return (w[0:1, :] * z_ref[pl.ds(HALO - 1, tm), :] + w[1:2, :] * z_ref[pl.ds(HALO, tm), :]
            + w[2:3, :] * z_ref[pl.ds(HALO + 1, tm), :])


def _inproj_kernel(xm_ref, xp_ref, xn_ref, mod_ref, g0_ref, cos_ref, sin_ref,
                   w_att_ref, w_ret_ref, w_dn_ref, w_dnz_ref, w_ab_ref, w_sc_ref, w_scb_ref,
                   g64_ref, attgain_ref, dnconv_ref, scconv_ref,
                   hb_ref, q_ref, kd_ref, vd_ref, ret_ref, dn_ref, dnz_ref, ab_ref, sc_ref,
                   zdn_ref, zsc_ref, *, tm, ctx_tiles, n_tiles):
    t = pl.program_id(0)
    prev_ok = jnp.logical_and(t != 0, t != ctx_tiles).astype(F32)
    next_ok = jnp.logical_and(t != ctx_tiles - 1, t != n_tiles - 1).astype(F32)

    x_ext = jnp.concatenate([xp_ref[...], xm_ref[...], xn_ref[...]], axis=0)
    ms = jnp.mean(x_ext * x_ext, axis=-1, keepdims=True)
    mod = mod_ref[...]
    sh1 = mod[:, 0:D_MODEL]
    sc1 = mod[:, D_MODEL:2 * D_MODEL]
    h_ext = (x_ext * lax.rsqrt(ms + EPS) * g0_ref[...]) * (1.0 + sc1) + sh1
    r = lax.broadcasted_iota(jnp.int32, (tm + 2 * HALO, 1), 0)
    h_ext = h_ext * jnp.where(r < HALO, prev_ok, jnp.where(r >= tm + HALO, next_ok, 1.0))
    hb_ext = h_ext.astype(BF16)
    hb = h_ext[HALO:HALO + tm, :].astype(BF16)
    hb_ref[...] = hb

    z = jnp.dot(hb, w_att_ref[...], preferred_element_type=F32)
    qk = z[:, 0:768]
    sq_hi, sq_lo = _split2(qk * qk)
    g64 = g64_ref[...]
    ms64 = (jnp.dot(sq_hi, g64, preferred_element_type=F32)
            + jnp.dot(sq_lo, g64, preferred_element_type=F32))
    qk = qk * lax.rsqrt(ms64 + EPS) * attgain_ref[...]
    cos = cos_ref[...]
    sin = sin_ref[...]
    qk = qk * cos + _swap_halves64(qk) * sin
    q_ref[...] = qk[:, 0:512].astype(BF16)
    kd_ref[...] = qk[:, 512:768].astype(BF16)
    vd_ref[...] = z[:, 768:1024].astype(BF16)

    z = jnp.dot(hb, w_ret_ref[...], preferred_element_type=F32)
    rqk = z[:, 0:512]
    rqk = rqk * cos[:, 0:512] + _swap_halves64(rqk) * sin[:, 0:512]
    ret_ref[:, 0:512] = rqk.astype(BF16)
    ret_ref[:, 512:1536] = z[:, 512:1536].astype(BF16)

    zdn_ref[...] = jnp.dot(hb_ext, w_dn_ref[...], preferred_element_type=F32)
    a = _silu(_conv3(zdn_ref, dnconv_ref[...], tm))
    for h in range(2 * DN_HEADS):
        ah = a[:, h * LANES:(h + 1) * LANES]
        ah = ah * lax.rsqrt(jnp.sum(ah * ah, axis=-1, keepdims=True) + EPS)
        if h < DN_HEADS:
            ah = ah * (DN_HEAD_DIM ** -0.5)
        dn_ref[:, h * LANES:(h + 1) * LANES] = ah.astype(BF16)
    dn_ref[:, 1024:1536] = a[:, 1024:1536].astype(BF16)
    dnz_ref[...] = jnp.dot(hb, w_dnz_ref[...], preferred_element_type=F32).astype(BF16)
    ab_ref[...] = jnp.dot(hb, w_ab_ref[...], preferred_element_type=F32)

    zc = jnp.dot(hb_ext, w_sc_ref[...], preferred_element_type=F32)
    zsc_ref[...] = zc[:, 0:512] * zc[:, 512:1024]
    bgate = jnp.dot(hb, w_scb_ref[...], preferred_element_type=F32)
    sc_ref[...] = (bgate * _conv3(zsc_ref, scconv_ref[...], tm)).astype(BF16)


def _inproj(x, mod4, g0, cos, sin, wts, *, tm, ctx_len):
    b, l, _ = x.shape
    n_tiles = l // tm
    ctx_tiles = ctx_len // tm
    hpt = tm // HALO
    last_halo = l // HALO - 1

    def tile(w):
        return pl.BlockSpec((None, tm, w), lambda t, i: (i, t, 0))

    in_specs = [
        tile(D_MODEL),
        pl.BlockSpec((None, HALO, D_MODEL), lambda t, i: (i, jnp.maximum(t * hpt - 1, 0), 0)),
        pl.BlockSpec((None, HALO, D_MODEL),
                     lambda t, i: (i, jnp.minimum((t + 1) * hpt, last_halo), 0)),
        pl.BlockSpec((None, None, 1, 6 * D_MODEL),
                     lambda t, i: (i, jnp.where(t >= ctx_tiles, 1, 0), 0, 0)),
        _const_spec((1, D_MODEL)),
        pl.BlockSpec((tm, 768), lambda t, i: (t, 0)),
        pl.BlockSpec((tm, 768), lambda t, i: (t, 0)),
    ] + [_const_spec(w.shape) for w in wts]
    widths = (1024, 512, 256, 256, 1536, 1536, 512)
    out_shape = [jax.ShapeDtypeStruct((b, l, w), BF16) for w in widths]
    out_specs = [tile(w) for w in widths]
    out_shape.insert(7, jax.ShapeDtypeStruct((b, l, 256), F32))
    out_specs.insert(7, tile(256))
    out_shape.append(jax.ShapeDtypeStruct((b, l, 512), BF16))
    out_specs.append(tile(512))
    return pl.pallas_call(
        functools.partial(_inproj_kernel, tm=tm, ctx_tiles=ctx_tiles, n_tiles=n_tiles),
        out_shape=out_shape,
        grid=(n_tiles, b),
        in_specs=in_specs,
        out_specs=out_specs,
        scratch_shapes=[pltpu.VMEM((tm + 2 * HALO, 1536), F32),
                        pltpu.VMEM((tm + 2 * HALO, 512), F32)],
        compiler_params=_params(("arbitrary", "arbitrary")),
        name="inproj",
    )(x, x, x, mod4, g0, cos, sin, *wts)


def _attn_kernel(q_ref, k_ref, v_ref, o_ref, *, ctx_tiles, ctx_len):
    t = pl.program_id(1)
    lane = lax.broadcasted_iota(jnp.int32, (1, LANES), 1)
    low = lane < 64

    def run(n_keys):
        for g in range(ATT_KV_HEADS):
            kd = k_ref[0:n_keys, g * LANES:(g + 1) * LANES]
            vd = v_ref[0:n_keys, g * LANES:(g + 1) * LANES]
            for j in range(2):
                c0 = (2 * g + j) * LANES
                qp = q_ref[:, c0:c0 + LANES]
                outs = []
                for e in range(2):
                    qm = jnp.where(low if e == 0 else jnp.logical_not(low), qp, jnp.zeros_like(qp))
                    s = _dot_nt(qm, kd)
                    m = jnp.max(s, axis=-1, keepdims=True)
                    p = jnp.exp(s - m)
                    den = jnp.sum(p, axis=-1, keepdims=True)
                    outs.append(_dot(p, vd) / den)
                o_ref[:, c0:c0 + LANES] = jnp.where(low, outs[0], outs[1]).astype(BF16)

    @pl.when(t < ctx_tiles)
    def _():
        run(ctx_len)

    @pl.when(t >= ctx_tiles)
    def _():
        run(k_ref.shape[0])


def _attention(q, kd, vd, *, tq, ctx_len, first_tile):
    b, l, _ = q.shape
    n_tiles = l // tq - first_tile
    ctx_tiles = ctx_len // tq - first_tile
    return pl.pallas_call(
        functools.partial(_attn_kernel, ctx_tiles=ctx_tiles, ctx_len=ctx_len),
        out_shape=jax.ShapeDtypeStruct((b, l, 512), BF16),
        grid=(b, n_tiles),
        in_specs=[pl.BlockSpec((None, tq, 512), lambda i, t: (i, t + first_tile, 0)),
                  pl.BlockSpec((None, l, 256), lambda i, t: (i, 0, 0)),
                  pl.BlockSpec((None, l, 256), lambda i, t: (i, 0, 0))],
        out_specs=pl.BlockSpec((None, tq, 512), lambda i, t: (i, t + first_tile, 0)),
        compiler_params=_params(("arbitrary", "arbitrary")),
        name="attention",
    )(q, kd, vd)


def _chunk_of_step(s, direction, ctx_chunks, n_chunks):
    if direction == 0:
        return s
    return jnp.where(s < ctx_chunks, ctx_chunks - 1 - s, n_chunks - 1 + ctx_chunks - s)


def _tri_masks(n, block):
    i = lax.broadcasted_iota(jnp.int32, (n, n), 0)
    j = lax.broadcasted_iota(jnp.int32, (n, n), 1)
    same = (i // block) == (j // block)
    return (jnp.logical_and(same, j <= i), jnp.logical_and(same, j < i),
            jnp.logical_and(same, j >= i), jnp.logical_and(same, j > i))


def _block_mask(n, block):
    i = lax.broadcasted_iota(jnp.int32, (n, n), 0)
    j = lax.broadcasted_iota(jnp.int32, (n, n), 1)
    return (i // block) == (j // block)


def _unit_tri_inverse(nmat, mm):
    n = nmat.shape[0]
    m16, m32 = _block_mask(n, 16), _block_mask(n, 32)
    eye = (lax.broadcasted_iota(jnp.int32, (n, n), 0)
           == lax.broadcasted_iota(jnp.int32, (n, n), 1)).astype(F32)
    n1 = jnp.where(m16, nmat, 0.0)
    p = eye - n1
    n2 = mm(n1, n1)
    p = p + mm(p, n2)
    n4 = mm(n2, n2)
    p = p + mm(p, n4)
    n8 = mm(n4, n4)
    p = p + mm(p, n8)
    c1 = jnp.where(jnp.logical_and(m32, jnp.logical_not(m16)), nmat, 0.0)
    p = p - mm(mm(p, c1), p)
    c2 = jnp.where(m32, 0.0, nmat)
    p = p - mm(mm(p, c2), p)
    return p


def _dn_kernel(q_ref, k_ref, v_ref, z_ref, ab_ref, alog_ref, dtb_ref, gain_ref, y_ref,
               of_ref, ob_ref, gl_ref, s_ref, *, ctx_chunks, n_chunks):
    c2 = 2 * CHUNK
    lane = lax.broadcasted_iota(jnp.int32, (1, LANES), 1)
    ab = ab_ref[...]
    a_coef = -jnp.exp(alog_ref[...])
    gl_ref[...] = jnp.where(lane < 4, a_coef * _softplus(ab + dtb_ref[...]), jax.nn.sigmoid(ab))
    s_ref[...] = jnp.zeros_like(s_ref)

    lo_incl, lo_strict, up_incl, up_strict = _tri_masks(c2, CHUNK)
    i64 = lax.broadcasted_iota(jnp.int32, (CHUNK, CHUNK), 0)
    j64 = lax.broadcasted_iota(jnp.int32, (CHUNK, CHUNK), 1)
    ones = jnp.ones((CHUNK, CHUNK), F32)
    cum_f = jnp.concatenate([(j64 <= i64).astype(F32), (j64 > i64).astype(F32), ones], 0).astype(BF16)
    cum_b = jnp.concatenate([(j64 >= i64).astype(F32), (j64 < i64).astype(F32), ones], 0).astype(BF16)

    def stack(ref, rows):
        return jnp.concatenate([ref[rows, 0:LANES], ref[rows, LANES:2 * LANES]], axis=0)

    def col(mat, idx):
        return jnp.concatenate([mat[:, idx:idx + 1], mat[:, idx + 1:idx + 2]], axis=0)

    def step(s, carry):
        for d in range(2):
            c = _chunk_of_step(s, d, ctx_chunks, n_chunks)
            rows = pl.ds(pl.multiple_of(c * CHUNK, CHUNK), CHUNK)
            incl, strict = (lo_incl, lo_strict) if d == 0 else (up_incl, up_strict)
            glc = gl_ref[rows, :]
            cums = _dot_exact_lhs(cum_f if d == 0 else cum_b, glc)
            g_col = col(cums[0:CHUNK], 2 * d)
            rest_col = col(cums[CHUNK:2 * CHUNK], 2 * d)
            tot = cums[2 * CHUNK:3 * CHUNK]
            beta_col = col(glc, 4 + 2 * d)
            g_mat = jnp.broadcast_to(g_col, (c2, c2))
            diff = g_mat - g_mat.T
            dec = jnp.where(incl, jnp.exp(jnp.where(incl, diff, 0.0)), 0.0)
            qp, kp, vp = stack(q_ref, rows), stack(k_ref, rows), stack(v_ref, rows)
            kk = _dot_nt(kp, kp)
            qk = _dot_nt(qp, kp)
            nmat = jnp.where(strict, beta_col * kk * dec, 0.0)
            a_qk = qk * dec
            t_inv = _unit_tri_inverse(nmat, _dot_x3)
            kf = kp.astype(F32)
            e_g = jnp.exp(g_col)
            rhs = jnp.concatenate([beta_col * vp.astype(F32), (beta_col * e_g) * kf], axis=1)
            w_sol = _dot_x3(t_inv, rhs)
            q_dec = qp.astype(F32) * e_g
            k_dec = kf * jnp.exp(rest_col)
            s_decay = []
            for h in range(2):
                th = tot[:, 2 * d + h:2 * d + h + 1]
                s_decay.append(jnp.exp(jnp.broadcast_to(jnp.concatenate([th, th], axis=0), (c2, LANES))))
            us = []
            for h in range(2):
                hr = slice(h * CHUNK, (h + 1) * CHUNK)
                us.append(w_sol[hr, 0:LANES] - _dot(w_sol[hr, LANES:2 * LANES], s_ref[d, h]))
            o_intra = _dot(a_qk, jnp.concatenate(us, axis=0))
            o_ref = of_ref if d == 0 else ob_ref
            for h in range(2):
                hr = slice(h * CHUNK, (h + 1) * CHUNK)
                s_old = s_ref[d, h]
                o_ref[rows, h * LANES:(h + 1) * LANES] = o_intra[hr] + _dot(q_dec[hr], s_old)
                s_ref[d, h] = s_old * s_decay[h] + _dot_tn(k_dec[hr], us[h])
        return carry

    lax.fori_loop(0, n_chunks, step, 0)

    gain = gain_ref[...]
    rb = math.gcd(n_chunks, 4) * CHUNK
    for i in range(n_chunks * CHUNK // rb):
        rows = slice(i * rb, (i + 1) * rb)
        for h in range(2):
            cols = slice(h * LANES, (h + 1) * LANES)
            o = of_ref[rows, cols] + ob_ref[rows, cols]
            o = o * lax.rsqrt(jnp.mean(o * o, axis=-1, keepdims=True) + EPS) * gain
            y_ref[rows, cols] = (o * _silu(z_ref[rows, cols].astype(F32))).astype(BF16)


def _deltanet(dn, dnz, ab, alog_rows, dtb_rows, gain, *, ctx_len):
    b, l, _ = dn.shape
    n_chunks = l // CHUNK

    def blk(off):
        return pl.BlockSpec((None, l, 256), lambda i, p: (i, 0, off + p))

    return pl.pallas_call(
        functools.partial(_dn_kernel, ctx_chunks=ctx_len // CHUNK, n_chunks=n_chunks),
        out_shape=jax.ShapeDtypeStruct((b, l, 512), BF16),
        grid=(b, 2),
        in_specs=[blk(0), blk(2), blk(4), blk(0),
                  pl.BlockSpec((None, l, LANES), lambda i, p: (i, 0, p)),
                  pl.BlockSpec((None, 1, LANES), lambda i, p: (p, 0, 0)),
                  pl.BlockSpec((None, 1, LANES), lambda i, p: (p, 0, 0)),
                  _const_spec((1, LANES))],
        out_specs=blk(0),
        scratch_shapes=[pltpu.VMEM((l, 256), F32), pltpu.VMEM((l, 256), F32),
                        pltpu.VMEM((l, LANES), F32), pltpu.VMEM((2, 2, LANES, LANES), F32)],
        compiler_params=_params(("arbitrary", "arbitrary")),
        name="deltanet",
    )(dn, dn, dn, dnz, ab, alog_rows, dtb_rows, gain)


def _ret_kernel(q_ref, k_ref, v_ref, g_ref, dk_ref, ds_ref, y_ref, of_ref, ob_ref, s_ref,
                *, ctx_chunks, n_chunks):
    lane = lax.broadcasted_iota(jnp.int32, (1, LANES), 1)
    low = lane < 64
    lane2 = lax.broadcasted_iota(jnp.int32, (1, 2 * LANES), 1)
    low2 = lane2 < LANES
    lg = -jnp.exp(dk_ref[...])
    lg_f, lg_b = lg[0:1, :], lg[1:2, :]
    i = lax.broadcasted_iota(jnp.int32, (CHUNK, LANES), 0).astype(F32)
    j = (lax.broadcasted_iota(jnp.int32, (CHUNK, LANES), 1) % CHUNK).astype(F32)
    rel = i - j
    dmask = (jnp.where(rel >= 0, jnp.exp(jnp.where(rel >= 0, rel, 0.0) * lg_f), 0.0)
             + jnp.where(rel <= 0, jnp.exp(jnp.where(rel <= 0, -rel, 0.0) * lg_b), 0.0))
    q_fac = (jnp.exp((i + 1.0) * lg_f), jnp.exp((CHUNK - i) * lg_b))
    k_fac = (jnp.exp((CHUNK - 1.0 - i) * lg_f), jnp.exp(i * lg_b))
    ri = lax.broadcasted_iota(jnp.int32, (LANES, 2 * LANES), 0)
    ci = lax.broadcasted_iota(jnp.int32, (LANES, 2 * LANES), 1)
    diag = (ri // CHUNK) == (ci // LANES)
    s_lg = -jnp.exp(ds_ref[...])
    s_decay = (jnp.where(diag, jnp.exp(CHUNK * s_lg[0]), 0.0),
               jnp.where(diag, jnp.exp(CHUNK * s_lg[1]), 0.0))
    s_ref[...] = jnp.zeros_like(s_ref)

    def step(s, carry):
        for d in range(2):
            c = _chunk_of_step(s, d, ctx_chunks, n_chunks)
            rows = pl.ds(pl.multiple_of(c * CHUNK, CHUNK), CHUNK)
            qc = q_ref[rows, :].astype(F32)
            kc = k_ref[rows, :].astype(F32)
            vc = v_ref[rows, :]
            s_old = s_ref[d]
            o = _dot(qc * q_fac[d], s_old)
            if d == 0:
                k_rs = jnp.concatenate([jnp.where(low, kc, 0.0), jnp.where(low, 0.0, kc)], axis=0)
                scores = _dot_nt(qc, k_rs) * dmask
                zero = jnp.zeros_like(vc)
                v_bd = jnp.concatenate([jnp.where(low2, vc, zero), jnp.where(low2, zero, vc)], 0)
                o = o + _dot(scores, v_bd)
                of_ref[rows, :] = o
            else:
                ob_ref[rows, :] = o
            kv = _dot_tn(kc * k_fac[d], vc)
            s_ref[d] = s_old * s_decay[d] + jnp.where(diag, kv, 0.0)
        return carry

    lax.fori_loop(0, n_chunks, step, 0)

    rb = math.gcd(n_chunks, 4) * CHUNK
    for b in range(n_chunks * CHUNK // rb):
        rows = slice(b * rb, (b + 1) * rb)
        for h in range(2):
            cols = slice(h * LANES, (h + 1) * LANES)
            o = of_ref[rows, cols] + ob_ref[rows, cols]
            mu = jnp.mean(o, axis=-1, keepdims=True)
            var = jnp.mean(jnp.square(o - mu), axis=-1, keepdims=True)
            y = (o - mu) * lax.rsqrt(var + EPS)
            y_ref[rows, cols] = (y * _silu(g_ref[rows, cols].astype(F32))).astype(BF16)


def _retention(ret, dk_rows, ds_mats, *, ctx_len):
    b, l, _ = ret.shape
    n_chunks = l // CHUNK
    return pl.pallas_call(
        functools.partial(_ret_kernel, ctx_chunks=ctx_len // CHUNK, n_chunks=n_chunks),
        out_shape=jax.ShapeDtypeStruct((b, l, 512), BF16),
        grid=(b, 2),
        in_specs=[pl.BlockSpec((None, l, LANES), lambda i, p: (i, 0, p)),
                  pl.BlockSpec((None, l, LANES), lambda i, p: (i, 0, 2 + p)),
                  pl.BlockSpec((None, l, 256), lambda i, p: (i, 0, 2 + p)),
                  pl.BlockSpec((None, l, 256), lambda i, p: (i, 0, 4 + p)),
                  pl.BlockSpec((None, 2, LANES), lambda i, p: (p, 0, 0)),
                  pl.BlockSpec((None, 2, LANES, 256), lambda i, p: (p, 0, 0, 0))],
        out_specs=pl.BlockSpec((None, l, 256), lambda i, p: (i, 0, p)),
        scratch_shapes=[pltpu.VMEM((l, 256), F32), pltpu.VMEM((l, 256), F32),
                        pltpu.VMEM((2, LANES, 256), F32)],
        compiler_params=_params(("arbitrary", "arbitrary")),
        name="retention",
    )(ret, ret, ret, ret, dk_rows, ds_mats)


def _merge_kernel(hb_ref, ya_ref, yd_ref, yr_ref, ys_ref, x_ref, mod_ref, wg_ref, wb_ref, wo_ref,
                  g1_ref, o_ref):
    hb = hb_ref[...]
    u = None
    for i, y_ref in enumerate((ya_ref, yd_ref, yr_ref, ys_ref)):
        gate = jax.nn.sigmoid(jnp.dot(hb, wg_ref[:, i * D_MODEL:(i + 1) * D_MODEL],
                                      preferred_element_type=F32))
        term = gate * jnp.dot(y_ref[...], wb_ref[i], preferred_element_type=F32)
        u = term if u is None else u + term
    y = _dot(u, wo_ref[...])
    y = y * lax.rsqrt(jnp.mean(y * y, axis=-1, keepdims=True) + EPS) * g1_ref[...]
    ga1 = mod_ref[:, 2 * D_MODEL:3 * D_MODEL]
    o_ref[...] = x_ref[...] + ga1 * y


def _merge(hb, ys, x, mod4, wg, wb, wo, g1, *, tm, ctx_len, first_tile):
    b, l, _ = x.shape
    n_tiles = l // tm - first_tile
    ctx_tiles = ctx_len // tm

    def tile(w):
        return pl.BlockSpec((None, tm, w), lambda t, i: (i, t + first_tile, 0))

    return pl.pallas_call(
        _merge_kernel,
        out_shape=jax.ShapeDtypeStruct((b, n_tiles * tm, D_MODEL), F32),
        grid=(n_tiles, b),
        in_specs=[tile(D_MODEL), tile(512), tile(512), tile(512), tile(512), tile(D_MODEL),
                  pl.BlockSpec((None, None, 1, 6 * D_MODEL),
                               lambda t, i: (i, jnp.where(t + first_tile >= ctx_tiles, 1, 0), 0, 0)),
                  _const_spec(wg.shape), _const_spec(wb.shape), _const_spec(wo.shape),
                  _const_spec((1, D_MODEL))],
        out_specs=pl.BlockSpec((None, tm, D_MODEL), lambda t, i: (i, t, 0)),
        compiler_params=_params(("arbitrary", "arbitrary")),
        name="merge",
    )(hb, *ys, x, mod4, wg, wb, wo, g1)


def _mlp_kernel(x_ref, mod_ref, g2_ref, g3_ref, w1_ref, w2_ref, o_ref, *, hid_tile):
    x = x_ref[...]
    mod = mod_ref[...]
    sh2 = mod[:, 3 * D_MODEL:4 * D_MODEL]
    sc2 = mod[:, 4 * D_MODEL:5 * D_MODEL]
    ga2 = mod[:, 5 * D_MODEL:6 * D_MODEL]
    h = (x * lax.rsqrt(jnp.mean(x * x, axis=-1, keepdims=True) + EPS) * g2_ref[...]) * (1.0 + sc2) + sh2
    hb = h.astype(BF16)
    acc = None
    for j in range(MLP_HIDDEN // hid_tile):
        cols = slice(j * hid_tile, (j + 1) * hid_tile)
        a = jnp.maximum(jnp.dot(hb, w1_ref[:, cols], preferred_element_type=F32), 0.0)
        part = _dot(a * a, w2_ref[cols, :])
        acc = part if acc is None else acc + part
    y = acc * lax.rsqrt(jnp.mean(acc * acc, axis=-1, keepdims=True) + EPS) * g3_ref[...]
    o_ref[...] = x + ga2 * y


def _mlp(x, mod4, g2, g3, w1, w2, *, tm, ctx_tiles):
    b, l, _ = x.shape
    n_tiles = l // tm
    return pl.pallas_call(
        functools.partial(_mlp_kernel, hid_tile=1024),
        out_shape=jax.ShapeDtypeStruct((b, l, D_MODEL), F32),
        grid=(n_tiles, b),
        in_specs=[pl.BlockSpec((None, tm, D_MODEL), lambda t, i: (i, t, 0)),
                  pl.BlockSpec((None, None, 1, 6 * D_MODEL),
                               lambda t, i: (i, jnp.where(t >= ctx_tiles, 1, 0), 0, 0)),
                  _const_spec((1, D_MODEL)), _const_spec((1, D_MODEL)),
                  _const_spec(w1.shape), _const_spec(w2.shape)],
        out_specs=pl.BlockSpec((None, tm, D_MODEL), lambda t, i: (i, t, 0)),
        compiler_params=_params(("arbitrary", "arbitrary")),
        name="mlp",
    )(x, mod4, g2, g3, w1, w2)


def _rope_tables(ctx_len, lat_len):
    rows = lat_len // GRID_W
    r, col = jnp.meshgrid(jnp.arange(rows), jnp.arange(GRID_W), indexing='ij')
    quarter = ATT_HEAD_DIM // 4
    inv_freq = ROPE_THETA ** (-jnp.arange(quarter, dtype=F32) / quarter)
    ang = jnp.concatenate([r.reshape(-1, 1).astype(F32) * inv_freq,
                           col.reshape(-1, 1).astype(F32) * inv_freq], axis=-1)
    cos = jnp.concatenate([jnp.ones((ctx_len, 32), F32), jnp.cos(ang)], axis=0)
    sin = jnp.concatenate([jnp.zeros((ctx_len, 32), F32), jnp.sin(ang)], axis=0)
    cos64 = jnp.concatenate([cos, cos], axis=-1)
    sin64 = jnp.concatenate([-sin, sin], axis=-1)
    return jnp.tile(cos64, (1, 12)), jnp.tile(sin64, (1, 12))


def _layer_weights(w_in, att_q_gain, att_k_gain, dn_a_log, dn_dt_bias, dn_norm_gain, ret_decay):
    o = _OFFS

    def cols(i):
        return w_in[:, o[i]:o[i + 1]]

    k, v = cols(1), cols(2)
    w_att = jnp.concatenate([cols(0), k[:, :64], k[:, :64], k[:, 64:], k[:, 64:],
                             v[:, :64], v[:, :64], v[:, 64:], v[:, 64:]], axis=1)
    w_ret = jnp.concatenate([cols(11), cols(12) * (RET_K_DIM ** -0.5), cols(13), cols(14)], axis=1)
    w_dn = jnp.concatenate([cols(3), cols(4), cols(5)], axis=1)
    w_dnz = cols(6)
    gates = [cols(7), cols(8), cols(9), cols(10)]
    ab_blocks = []
    for p in range(2):
        ab_blocks += [g[:, 2 * p:2 * p + 2] for g in gates]
        ab_blocks.append(jnp.zeros((D_MODEL, LANES - 8), w_in.dtype))
    w_ab = jnp.concatenate(ab_blocks, axis=1)
    w_sc = jnp.concatenate([cols(16), cols(17)], axis=1)
    w_scb = cols(15)
    w_gate = cols(18)
    wts = [w.astype(BF16) for w in (w_att, w_ret, w_dn, w_dnz, w_ab, w_sc, w_scb)]

    hid = np.arange(768) // 64
    g64 = jnp.asarray((hid[:, None] == hid[None, :]).astype(np.float32) / 64.0, BF16)
    attgain = jnp.concatenate([jnp.tile(att_q_gain * (ATT_HEAD_DIM ** -0.5), ATT_HEADS),
                               jnp.tile(att_k_gain, 4)]).reshape(1, 768)

    def pair_rows(t):
        rows = [jnp.concatenate([t[0, 2 * p:2 * p + 2], t[1, 2 * p:2 * p + 2],
                                 jnp.zeros((LANES - 4,), F32)]) for p in range(2)]
        return jnp.stack(rows).reshape(2, 1, LANES)

    alog_rows, dtb_rows = pair_rows(dn_a_log), pair_rows(dn_dt_bias)
    dn_gain = dn_norm_gain.reshape(1, LANES)
    dk_rows = jnp.stack([jnp.repeat(ret_decay[:, 2 * p:2 * p + 2], CHUNK, axis=1) for p in range(2)])
    ds_mats = jnp.stack([jnp.broadcast_to(jnp.repeat(ret_decay[:, 2 * p:2 * p + 2], CHUNK, axis=1)[:, :, None],
                                          (2, LANES, 2 * LANES)) for p in range(2)])
    return wts, w_gate.astype(BF16), g64, attgain, alog_rows, dtb_rows, dn_gain, dk_rows, ds_mats


def _forward(x, c, ctx, c_ctx, w_mod, b_mod, g_norm, w_in, att_q_gain, att_k_gain, dn_conv,
             dn_a_log, dn_dt_bias, dn_norm_gain, ret_decay, sc_conv, w_branch, w_out,
             w_mlp_in, w_mlp_out, *, tm):
    depth = w_mod.shape[0]
    bsz, lat_len, _ = x.shape
    ctx_len = ctx.shape[1]
    assert ctx_len % tm == 0 and lat_len % tm == 0 and tm % CHUNK == 0
    ctx_tiles = ctx_len // tm

    rows = -(-(bsz + 1) // SUBLANES) * SUBLANES
    cc = jnp.zeros((rows, D_MODEL), F32).at[:bsz].set(c).at[bsz].set(c_ctx)
    mod = _modulation(cc, w_mod, b_mod)
    mod4 = jnp.stack([jnp.broadcast_to(mod[:, bsz:bsz + 1], (depth, bsz, 6 * D_MODEL)),
                      mod[:, :bsz]], axis=2).reshape(depth, bsz, 2, 1, 6 * D_MODEL)

    cos, sin = _rope_tables(ctx_len, lat_len)
    xs = jnp.concatenate([ctx, x], axis=1)

    for l in range(depth):
        last = l == depth - 1
        (wts, w_gate, g64, attgain, alog_rows, dtb_rows, dn_gain, dk_rows, ds_mats) = _layer_weights(
            w_in[l], att_q_gain[l], att_k_gain[l], dn_a_log[l], dn_dt_bias[l], dn_norm_gain[l],
            ret_decay[l])
        g = g_norm[l].reshape(4, 1, D_MODEL)
        hb, q, kd, vd, ret, dn, dnz, ab, ysc = _inproj(
            xs, mod4[l], g[0], cos, sin,
            wts + [g64, attgain, dn_conv[l], sc_conv[l]], tm=tm, ctx_len=ctx_len)
        first_tile = ctx_tiles if last else 0
        y_att = _attention(q, kd, vd, tq=tm, ctx_len=ctx_len, first_tile=first_tile)
        y_dn = _deltanet(dn, dnz, ab, alog_rows, dtb_rows, dn_gain, ctx_len=ctx_len)
        y_ret = _retention(ret, dk_rows, ds_mats, ctx_len=ctx_len)
        xs = _merge(hb, (y_att, y_dn, y_ret, ysc), xs, mod4[l], w_gate, w_branch[l].astype(BF16),
                    w_out[l].astype(BF16), g[1], tm=tm, ctx_len=ctx_len, first_tile=first_tile)
        xs = _mlp(xs, mod4[l], g[2], g[3], w_mlp_in[l].astype(BF16), w_mlp_out[l].astype(BF16),
                  tm=tm, ctx_tiles=0 if last else ctx_tiles)
    return xs


def kernel(x, c, ctx, c_ctx, w_mod, b_mod, g_norm, w_in, att_q_gain, att_k_gain, dn_conv, dn_a_log,
           dn_dt_bias, dn_norm_gain, ret_decay, sc_conv, w_branch, w_out, w_mlp_in, w_mlp_out):
    return _forward(x, c, ctx, c_ctx, w_mod, b_mod, g_norm, w_in, att_q_gain, att_k_gain, dn_conv,
                    dn_a_log, dn_dt_bias, dn_norm_gain, ret_decay, sc_conv, w_branch, w_out,
                    w_mlp_in, w_mlp_out, tm=256)
```
